```python
import jax, jax.numpy as jnp
from jax import lax
import numpy as np

D_MODEL = 4096
BATCH = 4
SEQ = 4096
DEPTH = 1

DN_HEADS = 16
DN_HEAD_DIM = 128
DN_WIDTH = DN_HEADS * DN_HEAD_DIM
DN_CONV = 4
DN_CHUNK = 64
SW_GROUPS = ((128, 1), (512, 4), (2048, 16))
N_SW_GROUPS = 3
SW_HEADS = 8
SW_HEAD_DIM = 128
SW_GROUP_WIDTH = SW_HEADS * SW_HEAD_DIM
ROPE_THETA = 500000.0
ROPE_DIM = SW_HEAD_DIM // 4
N_BRANCHES = 2
D_FF = 11008
FFN_CONV = 3
NORM_EPS = 1e-6
N_MOD = 6
IN_SPLITS = (3 * DN_WIDTH, DN_WIDTH, DN_HEADS, DN_HEADS, 3 * N_SW_GROUPS * SW_GROUP_WIDTH, N_BRANCHES * D_MODEL)
IN_WIDTH = 3 * DN_WIDTH + DN_WIDTH + 2 * DN_HEADS + 3 * N_SW_GROUPS * SW_GROUP_WIDTH + N_BRANCHES * D_MODEL

kernel_name = "hybrid_deltanet_dilated_swa_convffn"


def rms_norm(x, eps=NORM_EPS):
    xf = x.astype(jnp.float32)
    return (xf * lax.rsqrt(jnp.mean(xf * xf, -1, keepdims=True) + eps)).astype(x.dtype)


def l2_norm(x, eps=NORM_EPS):
    return x * lax.rsqrt(jnp.sum(x * x, -1, keepdims=True) + eps)


def causal_dwconv(x, w):
    k_w = w.shape[0]
    t_len = x.shape[1]
    xp = jnp.pad(x, ((0, 0), (k_w - 1, 0), (0, 0)))
    return sum(w[j] * xp[:, j:j + t_len] for j in range(k_w))


def partial_rope(x, positions):
    half = ROPE_DIM // 2
    inv_freq = ROPE_THETA ** (-(jnp.arange(half, dtype=jnp.float32) * 2.0 / ROPE_DIM))
    ang = positions.astype(jnp.float32)[..., None] * inv_freq
    cos = jnp.cos(ang)[:, :, None, :]
    sin = jnp.sin(ang)[:, :, None, :]
    xf = x.astype(jnp.float32)
    x1, x2, rest = xf[..., :half], xf[..., half:ROPE_DIM], xf[..., ROPE_DIM:]
    out = jnp.concatenate([x1 * cos - x2 * sin, x2 * cos + x1 * sin, rest], -1)
    return out.astype(x.dtype)


def dilated_window_attention(q, k, v, dilation, span):
    b, t_len, h, dh = q.shape
    n = t_len // dilation
    nb = -(-n // span)
    npad = nb * span
    bd = b * dilation

    def to_strided(t):
        t = t.reshape(b, n, dilation, h, dh).transpose(0, 2, 1, 3, 4).reshape(bd, n, h, dh)
        return jnp.pad(t, ((0, 0), (0, npad - n), (0, 0), (0, 0)))

    def band(t):
        tb = jnp.pad(t, ((0, 0), (span, 0), (0, 0), (0, 0))).reshape(bd, nb + 1, span, h, dh)
        return jnp.concatenate([tb[:, :-1], tb[:, 1:]], axis=2)

    qb = to_strided(q).reshape(bd, nb, span, h, dh)
    kb = band(to_strided(k))
    vb = band(to_strided(v))
    s = jnp.einsum('znqhd,znkhd->znhqk', qb, kb, preferred_element_type=jnp.float32) * (dh ** -0.5)
    qi = jnp.arange(span)[:, None]
    ki = jnp.arange(2 * span)[None, :]
    rel = span + qi - ki
    blk = jnp.arange(nb)[:, None, None]
    valid = (rel >= 0) & (rel <= span) & (blk * span + ki - span >= 0)
    s = jnp.where(valid[None, :, None], s, -jnp.inf)
    m = jnp.max(s, -1, keepdims=True)
    p = jnp.exp(s - m)
    den = jnp.sum(p, -1, keepdims=True)
    o = jnp.einsum('znhqk,znkhd->znqhd', (p / den).astype(v.dtype), vb)
    lse = (m + jnp.log(den))[..., 0]
    o = o.reshape(bd, npad, h, dh)[:, :n].reshape(b, dilation, n, h, dh)
    o = o.transpose(0, 2, 1, 3, 4).reshape(b, t_len, h, dh)
    lse = lse.transpose(0, 1, 3, 2).reshape(bd, npad, h)[:, :n].reshape(b, dilation, n, h)
    lse = lse.transpose(0, 2, 1, 3).reshape(b, t_len, h)
    return o, lse


def gated_delta_rule(q, k, v, beta, g):
    b, t_len, h, dk = q.shape
    dv = v.shape[-1]
    cs = DN_CHUNK
    n = t_len // cs

    def chunk_vec(t):
        return t.reshape(b, n, cs, h, t.shape[-1]).transpose(0, 1, 3, 2, 4)

    def chunk_scalar(t):
        return t.reshape(b, n, cs, h).transpose(0, 1, 3, 2)

    qc, kc, vc = chunk_vec(q), chunk_vec(k), chunk_vec(v)
    bc, gcum = chunk_scalar(beta), jnp.cumsum(chunk_scalar(g), -1)
    tri = jnp.tril(jnp.ones((cs, cs), bool))
    strict = jnp.tril(jnp.ones((cs, cs), bool), -1)
    gamma = jnp.exp(jnp.where(tri, gcum[..., :, None] - gcum[..., None, :], -jnp.inf))
    kbeta = kc * bc[..., None]
    a_mat = jnp.where(strict, jnp.einsum('bnhid,bnhjd->bnhij', kbeta, kc) * gamma, 0.0)
    lhs = jnp.eye(cs, dtype=jnp.float32) + a_mat
    u = lax.linalg.triangular_solve(lhs, vc * bc[..., None], left_side=True, lower=True)
    w = lax.linalg.triangular_solve(lhs, kbeta * jnp.exp(gcum)[..., None], left_side=True, lower=True)
    aqk = jnp.einsum('bnhid,bnhjd->bnhij', qc, kc) * gamma
    qdec = qc * jnp.exp(gcum)[..., None]
    glast = gcum[..., -1]
    kdec = kc * jnp.exp(glast[..., None] - gcum)[..., None]

    def step(state, xs):
        u_i, w_i, qd_i, aqk_i, kd_i, gl_i = xs
        v_new = u_i - jnp.einsum('bhck,bhkv->bhcv', w_i, state)
        o_i = jnp.einsum('bhck,bhkv->bhcv', qd_i, state) + jnp.einsum('bhij,bhjv->bhiv', aqk_i, v_new)
        state = state * jnp.exp(gl_i)[..., None, None] + jnp.einsum('bhck,bhcv->bhkv', kd_i, v_new)
        return state, o_i

    xs = tuple(jnp.moveaxis(t, 1, 0) for t in (u, w, qdec, aqk, kdec, glast))
    s0 = jnp.zeros((b, h, dk, dv), jnp.float32)
    _, o = lax.scan(step, s0, xs)
    return o.transpose(1, 0, 3, 2, 4).reshape(b, t_len, h, dv)


def setup_inputs(seed: int = 0) -> dict:
    key = jax.random.key(seed)
    ks = jax.random.split(key, 20)
    f32 = jnp.float32
    nrm = lambda k, shape, scale: jax.random.normal(k, shape, f32) * scale
    dt = jnp.exp(jax.random.uniform(ks[9], (DEPTH, DN_HEADS), f32, np.log(1e-3), np.log(1e-1)))
    return {
        "x": nrm(ks[0], (BATCH, SEQ, D_MODEL), 1.0),
        "c": nrm(ks[1], (BATCH, D_MODEL), 1.0),
        "positions": jnp.broadcast_to(jnp.arange(SEQ, dtype=jnp.int32), (BATCH, SEQ)),
        "w_ada": nrm(ks[2], (DEPTH, D_MODEL, N_MOD * D_MODEL), 0.5 * D_MODEL ** -0.5),
        "b_ada": nrm(ks[3], (DEPTH, N_MOD * D_MODEL), 0.02),
        "w_in": nrm(ks[4], (DEPTH, D_MODEL, IN_WIDTH), D_MODEL ** -0.5),
        "b_gate": nrm(ks[5], (DEPTH, N_BRANCHES * D_MODEL), 0.02),
        "conv_qkv": nrm(ks[6], (DEPTH, DN_CONV, 3 * DN_WIDTH), DN_CONV ** -0.5),
        "a_log": jnp.log(jax.random.uniform(ks[7], (DEPTH, DN_HEADS), f32, 1.0, 16.0)),
        "dt_bias": dt + jnp.log(-jnp.expm1(-dt)),
        "o_norm_gain": 1.0 + nrm(ks[10], (DEPTH, DN_HEAD_DIM), 0.02),
        "w_a_proj": nrm(ks[11], (DEPTH, DN_WIDTH, D_MODEL), DN_WIDTH ** -0.5),
        "q_norm_gain": 1.0 + nrm(ks[12], (DEPTH, SW_HEAD_DIM), 0.02),
        "k_norm_gain": 1.0 + nrm(ks[13], (DEPTH, SW_HEAD_DIM), 0.02),
        "w_b_proj": nrm(ks[14], (DEPTH, SW_GROUP_WIDTH, D_MODEL), SW_GROUP_WIDTH ** -0.5),
        "w_o": nrm(ks[15], (DEPTH, D_MODEL, D_MODEL), D_MODEL ** -0.5),
        "w_up": nrm(ks[16], (DEPTH, D_MODEL, 2 * D_FF), D_MODEL ** -0.5),
        "conv_ffn": nrm(ks[17], (DEPTH, FFN_CONV, 2 * D_FF), FFN_CONV ** -0.5),
        "w_down": nrm(ks[18], (DEPTH, D_FF, D_MODEL), D_FF ** -0.5),
    }


def reference(x, c, positions, w_ada, b_ada, w_in, b_gate, conv_qkv, a_log, dt_bias, o_norm_gain,
              w_a_proj, q_norm_gain, k_norm_gain, w_b_proj, w_o, w_up, conv_ffn, w_down):
    b, t_len, _ = x.shape
    split_at = np.cumsum(IN_SPLITS)[:-1].tolist()
    for layer in range(DEPTH):
        mod = (jax.nn.silu(c) @ w_ada[layer] + b_ada[layer])[:, None, :]
        shift_mix, scale_mix, gate_mix, shift_ffn, scale_ffn, gate_ffn = jnp.split(mod, N_MOD, axis=-1)

        h = rms_norm(x) * (1 + scale_mix) + shift_mix
        proj = h @ w_in[layer]
        qkv_a, z_a, b_a, a_a, qkv_b, gate_logits = jnp.split(proj, split_at, axis=-1)

        qkv_a = jax.nn.silu(causal_dwconv(qkv_a, conv_qkv[layer])).astype(jnp.float32)
        qa, ka, va = [t.reshape(b, t_len, DN_HEADS, DN_HEAD_DIM) for t in jnp.split(qkv_a, 3, axis=-1)]
        qa = l2_norm(qa) * (DN_HEAD_DIM ** -0.5)
        ka = l2_norm(ka)
        beta = jax.nn.sigmoid(b_a.astype(jnp.float32))
        g = -jnp.exp(a_log[layer].astype(jnp.float32)) * jax.nn.softplus(
            a_a.astype(jnp.float32) + dt_bias[layer].astype(jnp.float32))
        o_a = gated_delta_rule(qa, ka, va, beta, g).astype(x.dtype)
        o_a = rms_norm(o_a) * o_norm_gain[layer] * jax.nn.silu(z_a.reshape(b, t_len, DN_HEADS, DN_HEAD_DIM))
        y_a = o_a.reshape(b, t_len, DN_WIDTH) @ w_a_proj[layer]

        sw = qkv_b.reshape(b, t_len, N_SW_GROUPS, 3, SW_HEADS, SW_HEAD_DIM)
        outs, lses = [], []
        for gi, (window, dilation) in enumerate(SW_GROUPS):
            qg = partial_rope(rms_norm(sw[:, :, gi, 0]) * q_norm_gain[layer], positions)
            kg = partial_rope(rms_norm(sw[:, :, gi, 1]) * k_norm_gain[layer], positions)
            o_g, lse_g = dilated_window_attention(qg, kg, sw[:, :, gi, 2], dilation, window // dilation)
            outs.append(o_g)
            lses.append(lse_g)
        alpha = jax.nn.softmax(jnp.stack(lses, 0), axis=0)
        o_b = jnp.sum(alpha[..., None] * jnp.stack(outs, 0).astype(jnp.float32), axis=0).astype(x.dtype)
        y_b = o_b.reshape(b, t_len, SW_GROUP_WIDTH) @ w_b_proj[layer]

        gate_a, gate_b = jnp.split(jax.nn.sigmoid(gate_logits + b_gate[layer]), N_BRANCHES, axis=-1)
        x = x + gate_mix * ((gate_a * y_a + gate_b * y_b) @ w_o[layer])

        h = rms_norm(x) * (1 + scale_ffn) + shift_ffn
        up = causal_dwconv(h @ w_up[layer], conv_ffn[layer])
        u_gate, u_val = jnp.split(up, 2, axis=-1)
        x = x + gate_ffn * ((jax.nn.silu(u_gate) * u_val) @ w_down[layer])
    return x
```

```python
import functools
from typing import NamedTuple

import numpy as np
import jax
import jax.numpy as jnp
from jax import lax
from jax.experimental import pallas as pl
from jax.experimental.pallas import tpu as pltpu

F32 = jnp.float32
BF16 = jnp.bfloat16

LANES = 128
SUBLANES = 8
HEAD_DIM = LANES
NORM_EPS = 1e-6
ROPE_THETA = 500000.0
ROPE_DIM = HEAD_DIM // 4
N_MOD = 6
MIB = 1 << 20


class Cfg(NamedTuple):
    d_model: int
    dn_heads: int
    dn_conv: int
    dn_chunk: int
    sw_groups: tuple
    sw_heads: int
    d_ff: int
    ffn_conv: int
    tm: int
    tn: int
    tm_norm: int
    tn_mod: int
    delta_hb: int
    delta_tt: int
    delta_cu: int
    tm_ffn: int
    tf_ffn: int
    vmem_mib: int


PROD = Cfg(d_model=4096, dn_heads=16, dn_conv=4, dn_chunk=64,
           sw_groups=((128, 1), (512, 4), (2048, 16)), sw_heads=8, d_ff=11008, ffn_conv=3,
           tm=1024, tn=512, tm_norm=512, tn_mod=512, delta_hb=4, delta_tt=1024, delta_cu=2,
           tm_ffn=512, tf_ffn=256, vmem_mib=56)


def _cparams(sem, vmem_mib):
    return pltpu.CompilerParams(dimension_semantics=sem, vmem_limit_bytes=vmem_mib * MIB)


def _sigmoid(x):
    return 1.0 / (1.0 + jnp.exp(-x))


def _silu(x):
    return x * _sigmoid(x)


def _dot(a, b):
    return jnp.dot(a, b, preferred_element_type=F32)


def _dot_nt(a, b):
    return lax.dot_general(a, b, (((1,), (1,)), ((), ())), preferred_element_type=F32)


def _dot_tn(a, b):
    return lax.dot_general(a, b, (((0,), (0,)), ((), ())), preferred_element_type=F32)


def _mod_kernel(c_ref, w_ref, b_ref, o_ref):
    s = _silu(c_ref[...]).astype(BF16)
    o_ref[...] = _dot(s, w_ref[...].astype(BF16)) + b_ref[...]


def _mod(c_pad, w_ada, b_ada, cfg):
    rows, d = c_pad.shape
    n = w_ada.shape[1]
    tn = cfg.tn_mod
    return pl.pallas_call(
        _mod_kernel,
        out_shape=jax.ShapeDtypeStruct((rows, n), F32),
        grid=(n // tn,),
        in_specs=[pl.BlockSpec((rows, d), lambda j: (0, 0)),
                  pl.BlockSpec((d, tn), lambda j: (0, j)),
                  pl.BlockSpec((1, tn), lambda j: (0, j))],
        out_specs=pl.BlockSpec((rows, tn), lambda j: (0, j)),
        compiler_params=_cparams(("arbitrary",), cfg.vmem_mib),
        name="mod",
    )(c_pad, w_ada, b_ada)


def _norm_kernel(x_ref, mod_ref, o_ref, *, shift_idx, scale_idx):
    x = x_ref[...]
    ms = jnp.mean(x * x, axis=-1, keepdims=True)
    xn = x * lax.rsqrt(ms + NORM_EPS)
    scale = mod_ref[scale_idx:scale_idx + 1, :]
    shift = mod_ref[shift_idx:shift_idx + 1, :]
    o_ref[...] = (xn * (1.0 + scale) + shift).astype(o_ref.dtype)


def _norm_mod(x2, mod3, seq, shift_idx, scale_idx, cfg):
    n, d = x2.shape
    tm = cfg.tm_norm
    tpb = seq // tm
    return pl.pallas_call(
        functools.partial(_norm_kernel, shift_idx=shift_idx, scale_idx=scale_idx),
        out_shape=jax.ShapeDtypeStruct((n, d), BF16),
        grid=(n // tm,),
        in_specs=[pl.BlockSpec((tm, d), lambda i: (i, 0)),
                  pl.BlockSpec((None, N_MOD, d), lambda i: (i // tpb, 0, 0))],
        out_specs=pl.BlockSpec((tm, d), lambda i: (i, 0)),
        compiler_params=_cparams(("arbitrary",), cfg.vmem_mib),
        name="norm_mod",
    )(x2, mod3)


def _mm_act_kernel(h_ref, w_ref, *rest, act):
    if act == "sigmoid_bias":
        b_ref, o_ref = rest
        o_ref[...] = _sigmoid(_dot(h_ref[...], w_ref[...]) + b_ref[...]).astype(o_ref.dtype)
    else:
        (o_ref,) = rest
        o_ref[...] = _silu(_dot(h_ref[...], w_ref[...])).astype(o_ref.dtype)


def _mm_act(h, w, bias, act, cfg, name):
    n, d = h.shape
    cols = w.shape[1]
    tm, tn = cfg.tm, cfg.tn
    in_specs = [pl.BlockSpec((tm, d), lambda i, j: (i, 0)),
                pl.BlockSpec((d, tn), lambda i, j: (0, j))]
    args = [h, w]
    if bias is not None:
        in_specs.append(pl.BlockSpec((1, tn), lambda i, j: (0, j)))
        args.append(bias)
    return pl.pallas_call(
        functools.partial(_mm_act_kernel, act=act),
        out_shape=jax.ShapeDtypeStruct((n, cols), BF16),
        grid=(n // tm, cols // tn),
        in_specs=in_specs,
        out_specs=pl.BlockSpec((tm, tn), lambda i, j: (i, j)),
        compiler_params=_cparams(("arbitrary", "arbitrary"), cfg.vmem_mib),
        name=name,
    )(*args)


def _softplus(x):
    return jnp.maximum(x, 0.0) + jnp.log(1.0 + jnp.exp(-jnp.abs(x)))


def _in_bg_kernel(h_ref, w_ref, alog_ref, dtb_ref, o_ref, *, nh, cs):
    acc = _dot(h_ref[...], w_ref[...])
    tm = acc.shape[0]
    lane = lax.broadcasted_iota(jnp.int32, (1, LANES), 1)
    is_beta = lane < nh
    is_g = jnp.logical_and(lane >= nh, lane < 2 * nh)
    beta = _sigmoid(acc)
    g = -jnp.exp(alog_ref[...]) * _softplus(acc + dtb_ref[...])
    g = jnp.where(is_g, g, 0.0)
    row = lax.broadcasted_iota(jnp.int32, (cs, cs), 0)
    col = lax.broadcasted_iota(jnp.int32, (cs, cs), 1)
    tri = (row >= col).astype(F32)
    for c in range(tm // cs):
        sl = slice(c * cs, (c + 1) * cs)
        gc = jnp.dot(tri, g[sl], preferred_element_type=F32, precision=lax.Precision.HIGHEST)
        o_ref[sl, :] = jnp.where(is_beta, beta[sl], gc)


def _in_bg(h, w_ba, alog_row, dtb_row, cfg):
    n, d = h.shape
    tm = cfg.tm
    return pl.pallas_call(
        functools.partial(_in_bg_kernel, nh=cfg.dn_heads, cs=cfg.dn_chunk),
        out_shape=jax.ShapeDtypeStruct((n, LANES), F32),
        grid=(n // tm,),
        in_specs=[pl.BlockSpec((tm, d), lambda i: (i, 0)),
                  pl.BlockSpec((d, LANES), lambda i: (0, 0)),
                  pl.BlockSpec((1, LANES), lambda i: (0, 0)),
                  pl.BlockSpec((1, LANES), lambda i: (0, 0))],
        out_specs=pl.BlockSpec((tm, LANES), lambda i: (i, 0)),
        compiler_params=_cparams(("arbitrary",), cfg.vmem_mib),
        name="in_bg",
    )(h, w_ba, alog_row, dtb_row)


def _in_a_kernel(h_ref, w_ref, cw_ref, o_ref, buf, carry, *, tpb, n_q_tiles, n_qk_tiles, kc, qscale):
    i = pl.program_id(0)
    j = pl.program_id(1)
    tm, tn = o_ref.shape
    acc = _dot(h_ref[...], w_ref[...])

    @pl.when(i % tpb == 0)
    def _():
        carry[j] = jnp.zeros((SUBLANES, tn), F32)

    buf[0:SUBLANES, :] = carry[j]
    buf[SUBLANES:SUBLANES + tm, :] = acc
    carry[j] = buf[tm:tm + SUBLANES, :]
    y = cw_ref[0:1, :] * buf[pl.ds(SUBLANES - (kc - 1), tm), :]
    for k in range(1, kc):
        y = y + cw_ref[k:k + 1, :] * buf[pl.ds(SUBLANES - (kc - 1) + k, tm), :]
    y = _silu(y)
    for hh in range(tn // HEAD_DIM):
        yh = y[:, hh * HEAD_DIM:(hh + 1) * HEAD_DIM]
        r = lax.rsqrt(jnp.sum(yh * yh, axis=-1, keepdims=True) + NORM_EPS)
        mult = jnp.where(j < n_q_tiles, r * qscale, jnp.where(j < n_qk_tiles, r, 1.0))
        o_ref[:, hh * HEAD_DIM:(hh + 1) * HEAD_DIM] = (yh * mult).astype(o_ref.dtype)


def _in_a(h, w, conv_w, seq, cfg):
    n, d = h.shape
    cols = w.shape[1]
    width = cols // 3
    tm, tn = cfg.tm, cfg.tn
    nj = cols // tn
    return pl.pallas_call(
        functools.partial(_in_a_kernel, tpb=seq // tm, n_q_tiles=width // tn, n_qk_tiles=2 * width // tn,
                          kc=cfg.dn_conv, qscale=HEAD_DIM ** -0.5),
        out_shape=jax.ShapeDtypeStruct((n, cols), BF16),
        grid=(n // tm, nj),
        in_specs=[pl.BlockSpec((tm, d), lambda i, j: (i, 0)),
                  pl.BlockSpec((d, tn), lambda i, j: (0, j)),
                  pl.BlockSpec((cfg.dn_conv, tn), lambda i, j: (0, j))],
        out_specs=pl.BlockSpec((tm, tn), lambda i, j: (i, j)),
        scratch_shapes=[pltpu.VMEM((tm + SUBLANES, tn), F32),
                        pltpu.VMEM((nj, SUBLANES, tn), F32)],
        compiler_params=_cparams(("arbitrary", "arbitrary"), cfg.vmem_mib),
        name="in_a",
    )(h, w, conv_w)


def _rope_kernel(pos_ref, freq_ref, sgn_ref, cos_ref, sin_ref):
    ang = pos_ref[...].astype(F32) * freq_ref[...]
    cos_ref[...] = jnp.cos(ang)
    sin_ref[...] = jnp.sin(ang) * sgn_ref[...]


def _rope_tables(pos_col, cfg):
    n = pos_col.shape[0]
    half = ROPE_DIM // 2
    inv_freq = ROPE_THETA ** (-(np.arange(half, dtype=np.float32) * 2.0 / ROPE_DIM))
    freq = np.zeros((1, LANES), np.float32)
    freq[0, :half] = inv_freq
    freq[0, half:ROPE_DIM] = inv_freq
    sgn = np.zeros((1, LANES), np.float32)
    sgn[0, :half] = -1.0
    sgn[0, half:ROPE_DIM] = 1.0
    tm = cfg.tm
    return pl.pallas_call(
        _rope_kernel,
        out_shape=(jax.ShapeDtypeStruct((n, LANES), F32), jax.ShapeDtypeStruct((n, LANES), F32)),
        grid=(n // tm,),
        in_specs=[pl.BlockSpec((tm, 1), lambda i: (i, 0)),
                  pl.BlockSpec((1, LANES), lambda i: (0, 0)),
                  pl.BlockSpec((1, LANES), lambda i: (0, 0))],
        out_specs=(pl.BlockSpec((tm, LANES), lambda i: (i, 0)), pl.BlockSpec((tm, LANES), lambda i: (i, 0))),
        compiler_params=_cparams(("arbitrary",), cfg.vmem_mib),
        name="rope_tables",
    )(pos_col, jnp.asarray(freq), jnp.asarray(sgn))


def _in_b_kernel(h_ref, w_ref, cos_ref, sin_ref, qg_ref, kg_ref, o_ref, ybuf, *, dil, nheads):
    j = pl.program_id(1)
    tm = h_ref.shape[0]
    acc = _dot(h_ref[...], w_ref[...])

    @pl.when(j == 2)
    def _():
        for hh in range(nheads):
            ybuf[hh] = acc[:, hh * HEAD_DIM:(hh + 1) * HEAD_DIM]

    @pl.when(j < 2)
    def _():
        gain = jnp.where(j == 0, qg_ref[...], kg_ref[...])
        cosf = cos_ref[...]
        sinf = sin_ref[...]
        lane = lax.broadcasted_iota(jnp.int32, (1, LANES), 1)
        half = ROPE_DIM // 2
        for hh in range(nheads):
            xh = acc[:, hh * HEAD_DIM:(hh + 1) * HEAD_DIM]
            ms = jnp.mean(xh * xh, axis=-1, keepdims=True)
            xn = xh * lax.rsqrt(ms + NORM_EPS) * gain
            rot = jnp.where(lane < half, pltpu.roll(xn, LANES - half, 1), pltpu.roll(xn, half, 1))
            ybuf[hh] = xn * cosf + rot * sinf

    n = tm // dil
    for hh in range(nheads):
        if dil == 1:
            o_ref[hh, 0] = ybuf[hh].astype(o_ref.dtype)
        else:
            def body(r, c, hh=hh):
                o_ref[hh, r] = ybuf[hh, pl.ds(r, n, stride=dil), :].astype(o_ref.dtype)
                return c
            lax.fori_loop(0, dil, body, 0)


def _in_b(h, w, cosf, sinf, qg, kg, batch, seq, dil, cfg):
    n, d = h.shape
    nheads = cfg.sw_heads
    tn = nheads * HEAD_DIM
    tm = cfg.tm
    tpb = seq // tm
    return pl.pallas_call(
        functools.partial(_in_b_kernel, dil=dil, nheads=nheads),
        out_shape=jax.ShapeDtypeStruct((batch, 3, nheads, dil, seq // dil, HEAD_DIM), BF16),
        grid=(n // tm, 3),
        in_specs=[pl.BlockSpec((tm, d), lambda i, j: (i, 0)),
                  pl.BlockSpec((d, tn), lambda i, j: (0, j)),
                  pl.BlockSpec((tm, LANES), lambda i, j: (i, 0)),
                  pl.BlockSpec((tm, LANES), lambda i, j: (i, 0)),
                  pl.BlockSpec((1, LANES), lambda i, j: (0, 0)),
                  pl.BlockSpec((1, LANES), lambda i, j: (0, 0))],
        out_specs=pl.BlockSpec((None, None, nheads, dil, tm // dil, HEAD_DIM),
                               lambda i, j: (i // tpb, j, 0, 0, i % tpb, 0)),
        scratch_shapes=[pltpu.VMEM((nheads, tm, HEAD_DIM), F32)],
        compiler_params=_cparams(("arbitrary", "arbitrary"), cfg.vmem_mib),
        name=f"in_b_d{dil}",
    )(h, w, cosf, sinf, qg, kg)


def _unit_lower_inverse(a, cs):
    row = lax.broadcasted_iota(jnp.int32, (cs, cs), 0)
    col = lax.broadcasted_iota(jnp.int32, (cs, cs), 1)
    eye = (row == col).astype(F32)
    base = SUBLANES
    ad = jnp.where(row // base == col // base, a, 0.0)
    a2 = _dot(ad, ad)
    a4 = _dot(a2, a2)
    x = _dot(_dot(eye - ad, eye + a2), eye + a4)
    s = base
    while s < cs:
        off = jnp.logical_and(row // (2 * s) == col // (2 * s),
                              jnp.logical_and((row // s) % 2 == 1, (col // s) % 2 == 0))
        x = x - _dot(_dot(x, jnp.where(off, a, 0.0)), x)
        s *= 2
    return x


def _delta_kernel(q_ref, k_ref, v_ref, bg_ref, z_ref, gain_ref, o_ref,
                  s_ref, bgt_s, u_s, w_s, qd_s, kd_s, aqk_s, egl_s, *, hb, nh, cs, cu):
    hg = pl.program_id(1)
    t = pl.program_id(2)
    tt = q_ref.shape[0]
    nchunks = tt // cs

    @pl.when(t == 0)
    def _():
        s_ref[...] = jnp.zeros(s_ref.shape, F32)

    row = lax.broadcasted_iota(jnp.int32, (cs, cs), 0)
    col = lax.broadcasted_iota(jnp.int32, (cs, cs), 1)
    tril = row >= col
    strict = row > col
    lane = lax.broadcasted_iota(jnp.int32, (1, LANES), 1)

    def prep(ci, carry):
        for cc in range(cu):
            c = ci * cu + cc
            r0 = pl.multiple_of(c * cs, cs)
            bg = bg_ref[pl.ds(r0, cs), :]
            bgt_s[cc] = bg.T
            for hh in range(hb):
                head = hg * hb + hh
                cols = slice(hh * HEAD_DIM, (hh + 1) * HEAD_DIM)
                q = q_ref[pl.ds(r0, cs), cols].astype(F32)
                k = k_ref[pl.ds(r0, cs), cols].astype(F32)
                v = v_ref[pl.ds(r0, cs), cols].astype(F32)
                beta = jnp.sum(jnp.where(lane == head, bg, 0.0), axis=-1, keepdims=True)
                gc = jnp.sum(jnp.where(lane == nh + head, bg, 0.0), axis=-1, keepdims=True)
                gl = gc[cs - 1:cs, :]
                gcr = bgt_s[cc, pl.ds(nh + head, 1), :]
                eg = jnp.exp(gc)
                kb = k * beta
                gam = jnp.where(tril, jnp.exp(gc - gcr), 0.0)
                k16 = k.astype(BF16)
                a = jnp.where(strict, _dot_nt(kb.astype(BF16), k16) * gam, 0.0)
                aqk = _dot_nt(q.astype(BF16), k16) * gam
                x16 = _unit_lower_inverse(a, cs).astype(BF16)
                u_s[hh, pl.ds(r0, cs), :] = _dot(x16, (v * beta).astype(BF16))
                w_s[hh, pl.ds(r0, cs), :] = _dot(x16, (kb * eg).astype(BF16)).astype(BF16)
                qd_s[hh, pl.ds(r0, cs), :] = (q * eg).astype(BF16)
                kd_s[hh, pl.ds(r0, cs), :] = (k * jnp.exp(gl - gc)).astype(BF16)
                aqk_s[hh, pl.ds(r0, cs), :] = aqk.astype(BF16)
                egl_s[hh, c] = jnp.broadcast_to(jnp.exp(gl), (1, LANES))
        return carry

    lax.fori_loop(0, nchunks // cu, prep, 0)

    gain = gain_ref[...]

    def scan(c, carry):
        r0 = pl.multiple_of(c * cs, cs)
        for hh in range(hb):
            cols = slice(hh * HEAD_DIM, (hh + 1) * HEAD_DIM)
            state = s_ref[hh]
            s16 = state.astype(BF16)
            vn = u_s[hh, pl.ds(r0, cs), :] - _dot(w_s[hh, pl.ds(r0, cs), :], s16)
            vn16 = vn.astype(BF16)
            o = _dot(qd_s[hh, pl.ds(r0, cs), :], s16) + _dot(aqk_s[hh, pl.ds(r0, cs), :], vn16)
            s_ref[hh] = state * egl_s[hh, c] + _dot_tn(kd_s[hh, pl.ds(r0, cs), :], vn16)
            ms = jnp.mean(o * o, axis=-1, keepdims=True)
            on = o * lax.rsqrt(ms + NORM_EPS) * gain * z_ref[pl.ds(r0, cs), cols].astype(F32)
            o_ref[pl.ds(r0, cs), cols] = on.astype(o_ref.dtype)
        return carry

    lax.fori_loop(0, nchunks, scan, 0)


def _delta(qkv, bg, z, gain_row, batch, seq, cfg):
    n, cols3 = qkv.shape
    width = cols3 // 3
    hb, tt, cs, cu = cfg.delta_hb, cfg.delta_tt, cfg.dn_chunk, cfg.delta_cu
    bw = hb * HEAD_DIM
    ngroups = width // bw
    tpb = seq // tt
    nchunks = tt // cs
    blk = lambda off: pl.BlockSpec((tt, bw), lambda b, g, t: (b * tpb + t, off + g))
    return pl.pallas_call(
        functools.partial(_delta_kernel, hb=hb, nh=cfg.dn_heads, cs=cs, cu=cu),
        out_shape=jax.ShapeDtypeStruct((n, width), BF16),
        grid=(batch, ngroups, tpb),
        in_specs=[blk(0), blk(ngroups), blk(2 * ngroups),
                  pl.BlockSpec((tt, LANES), lambda b, g, t: (b * tpb + t, 0)),
                  blk(0),
                  pl.BlockSpec((1, LANES), lambda b, g, t: (0, 0))],
        out_specs=blk(0),
        scratch_shapes=[pltpu.VMEM((hb, HEAD_DIM, HEAD_DIM), F32),
                        pltpu.VMEM((cu, LANES, cs), F32),
                        pltpu.VMEM((hb, tt, HEAD_DIM), F32),
                        pltpu.VMEM((hb, tt, HEAD_DIM), BF16),
                        pltpu.VMEM((hb, tt, HEAD_DIM), BF16),
                        pltpu.VMEM((hb, tt, HEAD_DIM), BF16),
                        pltpu.VMEM((hb, tt, cs), BF16),
                        pltpu.VMEM((hb, nchunks, 1, LANES), F32)],
        compiler_params=_cparams(("arbitrary", "arbitrary", "arbitrary"), cfg.vmem_mib),
        name="delta_rule",
    )(qkv, qkv, qkv, bg, z, gain_row)


ATT_BLK = 128


def _attn_kernel(*refs, dils, seq):
    ng = len(dils)
    qkv_refs = refs[:3 * ng]
    o_ref = refs[3 * ng]
    nat_o, nat_l = refs[3 * ng + 1:]
    blk = ATT_BLK
    scale = HEAD_DIM ** -0.5
    row = lax.broadcasted_iota(jnp.int32, (blk, blk), 0)
    col = lax.broadcasted_iota(jnp.int32, (blk, blk), 1)
    mask_cur = col <= row
    mask_prev = col >= row
    neg = -jnp.inf

    for g, dil in enumerate(dils):
        q_ref, k_ref, v_ref = qkv_refs[3 * g:3 * g + 3]
        nb = seq // dil // blk

        def body(idx, carry, g=g, dil=dil, nb=nb, q_ref=q_ref, k_ref=k_ref, v_ref=v_ref):
            r = idx // nb
            b = idx % nb
            c0 = pl.multiple_of(b * blk, blk)
            p0 = pl.multiple_of(jnp.maximum(b - 1, 0) * blk, blk)
            q = q_ref[r, pl.ds(c0, blk), :]
            sc = _dot_nt(q, k_ref[r, pl.ds(c0, blk), :]) * scale
            sp = _dot_nt(q, k_ref[r, pl.ds(p0, blk), :]) * scale
            sc = jnp.where(mask_cur, sc, neg)
            sp = jnp.where(jnp.logical_and(mask_prev, b > 0), sp, neg)
            m = jnp.maximum(jnp.max(sc, axis=-1, keepdims=True), jnp.max(sp, axis=-1, keepdims=True))
            pc = jnp.exp(sc - m)
            pp = jnp.exp(sp - m)
            den = jnp.sum(pc, axis=-1, keepdims=True) + jnp.sum(pp, axis=-1, keepdims=True)
            inv = 1.0 / den
            o = (_dot((pc * inv).astype(BF16), v_ref[r, pl.ds(c0, blk), :])
                 + _dot((pp * inv).astype(BF16), v_ref[r, pl.ds(p0, blk), :]))
            lse = m + jnp.log(den)
            start = r + dil * c0
            if dil == 1:
                rows = pl.ds(pl.multiple_of(start, blk), blk)
            else:
                rows = pl.ds(start, blk, stride=dil)
            nat_o[g, rows, :] = o
            nat_l[g, rows, :] = jnp.broadcast_to(lse, (blk, LANES))
            return carry

        lax.fori_loop(0, dil * nb, body, 0)

    mrows = 256

    def merge(i, carry):
        rows = pl.ds(pl.multiple_of(i * mrows, mrows), mrows)
        ls = [nat_l[g, rows, :] for g in range(ng)]
        m = functools.reduce(jnp.maximum, ls)
        es = [jnp.exp(l - m) for l in ls]
        inv = 1.0 / functools.reduce(lambda a, b: a + b, es)
        o = functools.reduce(lambda a, b: a + b, [(es[g] * inv) * nat_o[g, rows, :] for g in range(ng)])
        o_ref[rows, :] = o.astype(o_ref.dtype)
        return carry

    lax.fori_loop(0, seq // mrows, merge, 0)


def _attn(qkvs, dils, batch, seq, cfg):
    nheads = cfg.sw_heads
    ng = len(dils)
    in_specs, args = [], []
    for arr, dil in zip(qkvs, dils):
        for s in range(3):
            in_specs.append(pl.BlockSpec((None, None, None, dil, seq // dil, HEAD_DIM),
                                         lambda b, h, s=s: (b, s, h, 0, 0, 0)))
            args.append(arr)
    return pl.pallas_call(
        functools.partial(_attn_kernel, dils=tuple(dils), seq=seq),
        out_shape=jax.ShapeDtypeStruct((batch * seq, nheads * HEAD_DIM), BF16),
        grid=(batch, nheads),
        in_specs=in_specs,
        out_specs=pl.BlockSpec((seq, HEAD_DIM), lambda b, h: (b, h)),
        scratch_shapes=[pltpu.VMEM((ng, seq, HEAD_DIM), F32),
                        pltpu.VMEM((ng, seq, LANES), F32)],
        compiler_params=_cparams(("arbitrary", "arbitrary"), cfg.vmem_mib),
        name="swa",
    )(*args)


def _merge_kernel(oa_ref, ob_ref, wa_ref, wb_ref, ga_ref, gb_ref, o_ref):
    ya = _dot(oa_ref[...], wa_ref[...])
    yb = _dot(ob_ref[...], wb_ref[...])
    o_ref[...] = (ga_ref[...].astype(F32) * ya + gb_ref[...].astype(F32) * yb).astype(o_ref.dtype)


def _merge(oa, ob, wa, wb, gate, cfg):
    n, ka = oa.shape
    kb = ob.shape[1]
    d = wa.shape[1]
    tm, tn = cfg.tm, cfg.tn
    nj = d // tn
    return pl.pallas_call(
        _merge_kernel,
        out_shape=jax.ShapeDtypeStruct((n, d), BF16),
        grid=(n // tm, nj),
        in_specs=[pl.BlockSpec((tm, ka), lambda i, j: (i, 0)),
                  pl.BlockSpec((tm, kb), lambda i, j: (i, 0)),
                  pl.BlockSpec((ka, tn), lambda i, j: (0, j)),
                  pl.BlockSpec((kb, tn), lambda i, j: (0, j)),
                  pl.BlockSpec((tm, tn), lambda i, j: (i, j)),
                  pl.BlockSpec((tm, tn), lambda i, j: (i, nj + j))],
        out_specs=pl.BlockSpec((tm, tn), lambda i, j: (i, j)),
        compiler_params=_cparams(("arbitrary", "arbitrary"), cfg.vmem_mib),
        name="merge",
    )(oa, ob, wa, wb, gate, gate)


def _oproj_kernel(m_ref, w_ref, x_ref, mod_ref, o_ref, *, gate_idx):
    y = _dot(m_ref[...], w_ref[...])
    o_ref[...] = x_ref[...] + mod_ref[gate_idx:gate_idx + 1, :] * y


def _oproj(m, w_o, x2, mod3, seq, gate_idx, cfg):
    n, d = m.shape
    tm, tn = cfg.tm, cfg.tn
    tpb = seq // tm
    return pl.pallas_call(
        functools.partial(_oproj_kernel, gate_idx=gate_idx),
        out_shape=jax.ShapeDtypeStruct((n, d), F32),
        grid=(n // tm, d // tn),
        in_specs=[pl.BlockSpec((tm, d), lambda i, j: (i, 0)),
                  pl.BlockSpec((d, tn), lambda i, j: (0, j)),
                  pl.BlockSpec((tm, tn), lambda i, j: (i, j)),
                  pl.BlockSpec((None, N_MOD, tn), lambda i, j: (i // tpb, 0, j))],
        out_specs=pl.BlockSpec((tm, tn), lambda i, j: (i, j)),
        compiler_params=_cparams(("arbitrary", "arbitrary"), cfg.vmem_mib),
        name="oproj",
    )(m, w_o, x2, mod3)


def _ffn_kernel(h_ref, wg_ref, wv_ref, cg_ref, cv_ref, wd_ref, x_ref, mod_ref, o_ref,
                bufg, bufv, carry, *, tpb, kc, gate_idx):
    i = pl.program_id(0)
    j = pl.program_id(1)
    nj = pl.num_programs(1)
    tm = h_ref.shape[0]
    tf = wg_ref.shape[1]
    h = h_ref[...]

    @pl.when(i % tpb == 0)
    def _():
        carry[j] = jnp.zeros((2, SUBLANES, tf), F32)

    def conv(buf, which, w_ref, c_ref):
        buf[0:SUBLANES, :] = carry[j, which]
        buf[SUBLANES:SUBLANES + tm, :] = _dot(h, w_ref[...])
        carry[j, which] = buf[tm:tm + SUBLANES, :]
        y = c_ref[0:1, :] * buf[pl.ds(SUBLANES - (kc - 1), tm), :]
        for k in range(1, kc):
            y = y + c_ref[k:k + 1, :] * buf[pl.ds(SUBLANES - (kc - 1) + k, tm), :]
        return y

    ug = conv(bufg, 0, wg_ref, cg_ref)
    uv = conv(bufv, 1, wv_ref, cv_ref)
    part = _dot((_silu(ug) * uv).astype(BF16), wd_ref[...])

    @pl.when(j == 0)
    def _():
        o_ref[...] = part

    @pl.when(j > 0)
    def _():
        o_ref[...] += part

    @pl.when(j == nj - 1)
    def _():
        o_ref[...] = x_ref[...] + mod_ref[gate_idx:gate_idx + 1, :] * o_ref[...]


def _ffn(h, w_up, conv_w, w_down, x1, mod3, seq, gate_idx, cfg):
    n, d = h.shape
    f = w_down.shape[0]
    tm, tf = cfg.tm_ffn, cfg.tf_ffn
    nj = f // tf
    tpb = seq // tm
    kc = cfg.ffn_conv
    return pl.pallas_call(
        functools.partial(_ffn_kernel, tpb=tpb, kc=kc, gate_idx=gate_idx),
        out_shape=jax.ShapeDtypeStruct((n, d), F32),
        grid=(n // tm, nj),
        in_specs=[pl.BlockSpec((tm, d), lambda i, j: (i, 0)),
                  pl.BlockSpec((d, tf), lambda i, j: (0, j)),
                  pl.BlockSpec((d, tf), lambda i, j: (0, nj + j)),
                  pl.BlockSpec((kc, tf), lambda i, j: (0, j)),
                  pl.BlockSpec((kc, tf), lambda i, j: (0, nj + j)),
                  pl.BlockSpec((tf, d), lambda i, j: (j, 0)),
                  pl.BlockSpec((tm, d), lambda i, j: (i, 0), pipeline_mode=pl.Buffered(1)),
                  pl.BlockSpec((None, N_MOD, d), lambda i, j: (i // tpb, 0, 0))],
        out_specs=pl.BlockSpec((tm, d), lambda i, j: (i, 0)),
        scratch_shapes=[pltpu.VMEM((tm + SUBLANES, tf), F32),
                        pltpu.VMEM((tm + SUBLANES, tf), F32),
                        pltpu.VMEM((nj, 2, SUBLANES, tf), F32)],
        compiler_params=_cparams(("arbitrary", "arbitrary"), cfg.vmem_mib),
        name="ffn",
    )(h, w_up, w_up, conv_w, conv_w, w_down, x1, mod3)


def _row(v, offset=0):
    return jnp.zeros((1, LANES), F32).at[0, offset:offset + v.shape[0]].set(v.astype(F32))


def _forward(cfg, x, c, positions, w_ada, b_ada, w_in, b_gate, conv_qkv, a_log, dt_bias, o_norm_gain,
             w_a_proj, q_norm_gain, k_norm_gain, w_b_proj, w_o, w_up, conv_ffn, w_down):
    batch, seq, d = x.shape
    n = batch * seq
    depth = w_ada.shape[0]
    nh = cfg.dn_heads
    dn_width = nh * HEAD_DIM
    sw_width = cfg.sw_heads * HEAD_DIM
    ng = len(cfg.sw_groups)
    dils = [dl for _, dl in cfg.sw_groups]
    assert all(win // dl == ATT_BLK for win, dl in cfg.sw_groups)
    assert 2 * nh <= LANES
    splits = np.cumsum([3 * dn_width, dn_width, nh, nh, 3 * ng * sw_width, 2 * d])
    assert w_in.shape[2] == splits[-1]

    x2 = x.reshape(n, d)
    pos_col = positions.reshape(n, 1)
    c_pad = jnp.zeros((2 * SUBLANES, d), F32).at[:batch].set(c)
    cosf, sinf = _rope_tables(pos_col, cfg)

    for layer in range(depth):
        wl = w_in[layer]
        w_qkv_a = wl[:, :splits[0]].astype(BF16)
        w_z = wl[:, splits[0]:splits[1]].astype(BF16)
        w_ba = jnp.zeros((d, LANES), BF16).at[:, :2 * nh].set(wl[:, splits[1]:splits[3]].astype(BF16))
        w_qkv_b = [wl[:, splits[3] + g * 3 * sw_width:splits[3] + (g + 1) * 3 * sw_width].astype(BF16)
                   for g in range(ng)]
        w_gate = wl[:, splits[4]:].astype(BF16)
        w_a16 = w_a_proj[layer].astype(BF16)
        w_b16 = w_b_proj[layer].astype(BF16)
        w_o16 = w_o[layer].astype(BF16)
        w_up16 = w_up[layer].astype(BF16)
        w_down16 = w_down[layer].astype(BF16)

        mod = _mod(c_pad, w_ada[layer], b_ada[layer][None, :], cfg)[:batch]
        mod3 = mod.reshape(batch, N_MOD, d)

        h = _norm_mod(x2, mod3, seq, 0, 1, cfg)
        qkv_a = _in_a(h, w_qkv_a, conv_qkv[layer], seq, cfg)
        bg = _in_bg(h, w_ba, _row(a_log[layer], nh), _row(dt_bias[layer], nh), cfg)
        z = _mm_act(h, w_z, None, "silu", cfg, "in_z")
        gate = _mm_act(h, w_gate, b_gate[layer][None, :], "sigmoid_bias", cfg, "in_gate")
        qg = _row(q_norm_gain[layer])
        kg = _row(k_norm_gain[layer])
        qkv_b = [_in_b(h, w_qkv_b[g], cosf, sinf, qg, kg, batch, seq, dils[g], cfg) for g in range(ng)]

        o_a = _delta(qkv_a, bg, z, _row(o_norm_gain[layer]), batch, seq, cfg)
        o_b = _attn(qkv_b, dils, batch, seq, cfg)

        m = _merge(o_a, o_b, w_a16, w_b16, gate, cfg)
        x2 = _oproj(m, w_o16, x2, mod3, seq, 2, cfg)
        h2 = _norm_mod(x2, mod3, seq, 3, 4, cfg)
        x2 = _ffn(h2, w_up16, conv_ffn[layer], w_down16, x2, mod3, seq, 5, cfg)
    return x2.reshape(batch, seq, d)


def kernel(x, c, positions, w_ada, b_ada, w_in, b_gate, conv_qkv, a_log, dt_bias, o_norm_gain, w_a_proj,
           q_norm_gain, k_norm_gain, w_b_proj, w_o, w_up, conv_ffn, w_down):
    return _forward(PROD, x, c, positions, w_ada, b_ada, w_in, b_gate, conv_qkv, a_log, dt_bias, o_norm_gain,
                    w_a_proj, q_norm_gain, k_norm_gain, w_b_proj, w_o, w_up, conv_ffn, w_down)
```

```python
import functools
from typing import NamedTuple

import numpy as np
import jax
import jax.numpy as jnp
from jax import lax
from jax.experimental import pallas as pl
from jax.experimental.pallas import tpu as pltpu

F32 = jnp.float32
BF16 = jnp.bfloat16

LANES = 128
SUBLANES = 8
HEAD_DIM = LANES
NORM_EPS = 1e-6
ROPE_THETA = 500000.0
ROPE_DIM = HEAD_DIM // 4
N_MOD = 6
MIB = 1 << 20


class Cfg(NamedTuple):
    d_model: int
    dn_heads: int
    dn_conv: int
    dn_chunk: int
    sw_groups: tuple
    sw_heads: int
    d_ff: int
    ffn_conv: int
    tm: int
    tn: int
    tm_norm: int
    tn_mod: int
    delta_hb: int
    delta_tt: int
    delta_cs: int
    delta_cu: int
    attn_gb: int
    tm_ffn: int
    tf_ffn: int
    vmem_mib: int


PROD = Cfg(d_model=4096, dn_heads=16, dn_conv=4, dn_chunk=64,
           sw_groups=((128, 1), (512, 4), (2048, 16)), sw_heads=8, d_ff=11008, ffn_conv=3,
           tm=1024, tn=512, tm_norm=512, tn_mod=512, delta_hb=4, delta_tt=1024, delta_cs=128, delta_cu=2,
           attn_gb=4, tm_ffn=512, tf_ffn=256, vmem_mib=56)


def _cparams(sem, vmem_mib):
    return pltpu.CompilerParams(dimension_semantics=sem, vmem_limit_bytes=vmem_mib * MIB)


def _sigmoid(x):
    return 1.0 / (1.0 + jnp.exp(-x))


def _silu(x):
    return x * _sigmoid(x)


def _dot(a, b):
    return jnp.dot(a, b, preferred_element_type=F32)


def _dot_nt(a, b):
    return lax.dot_general(a, b, (((1,), (1,)), ((), ())), preferred_element_type=F32)


def _dot_tn(a, b):
    return lax.dot_general(a, b, (((0,), (0,)), ((), ())), preferred_element_type=F32)


def _mod_kernel(c_ref, w_ref, b_ref, o_ref):
    s = _silu(c_ref[...]).astype(BF16)
    o_ref[...] = _dot(s, w_ref[...].astype(BF16)) + b_ref[...]


def _mod(c_pad, w_ada, b_ada, cfg):
    rows, d = c_pad.shape
    n = w_ada.shape[1]
    tn = cfg.tn_mod
    return pl.pallas_call(
        _mod_kernel,
        out_shape=jax.ShapeDtypeStruct((rows, n), F32),
        grid=(n // tn,),
        in_specs=[pl.BlockSpec((rows, d), lambda j: (0, 0)),
                  pl.BlockSpec((d, tn), lambda j: (0, j)),
                  pl.BlockSpec((1, tn), lambda j: (0, j))],
        out_specs=pl.BlockSpec((rows, tn), lambda j: (0, j)),
        compiler_params=_cparams(("arbitrary",), cfg.vmem_mib),
        name="mod",
    )(c_pad, w_ada, b_ada)


def _norm_kernel(x_ref, mod_ref, o_ref, *, shift_idx, scale_idx):
    x = x_ref[...]
    ms = jnp.mean(x * x, axis=-1, keepdims=True)
    xn = x * lax.rsqrt(ms + NORM_EPS)
    scale = mod_ref[scale_idx:scale_idx + 1, :]
    shift = mod_ref[shift_idx:shift_idx + 1, :]
    o_ref[...] = (xn * (1.0 + scale) + shift).astype(o_ref.dtype)


def _norm_mod(x2, mod3, seq, shift_idx, scale_idx, cfg):
    n, d = x2.shape
    tm = cfg.tm_norm
    tpb = seq // tm
    return pl.pallas_call(
        functools.partial(_norm_kernel, shift_idx=shift_idx, scale_idx=scale_idx),
        out_shape=jax.ShapeDtypeStruct((n, d), BF16),
        grid=(n // tm,),
        in_specs=[pl.BlockSpec((tm, d), lambda i: (i, 0)),
                  pl.BlockSpec((None, N_MOD, d), lambda i: (i // tpb, 0, 0))],
        out_specs=pl.BlockSpec((tm, d), lambda i: (i, 0)),
        compiler_params=_cparams(("arbitrary",), cfg.vmem_mib),
        name="norm_mod",
    )(x2, mod3)


def _mm_act_kernel(h_ref, w_ref, *rest, act):
    if act == "sigmoid_bias":
        b_ref, o_ref = rest
        o_ref[...] = _sigmoid(_dot(h_ref[...], w_ref[...]) + b_ref[...]).astype(o_ref.dtype)
    else:
        (o_ref,) = rest
        o_ref[...] = _silu(_dot(h_ref[...], w_ref[...])).astype(o_ref.dtype)


def _mm_act(h, w, bias, act, cfg, name):
    n, d = h.shape
    cols = w.shape[1]
    tm, tn = cfg.tm, cfg.tn
    in_specs = [pl.BlockSpec((tm, d), lambda i, j: (i, 0)),
                pl.BlockSpec((d, tn), lambda i, j: (0, j))]
    args = [h, w]
    if bias is not None:
        in_specs.append(pl.BlockSpec((1, tn), lambda i, j: (0, j)))
        args.append(bias)
    return pl.pallas_call(
        functools.partial(_mm_act_kernel, act=act),
        out_shape=jax.ShapeDtypeStruct((n, cols), BF16),
        grid=(n // tm, cols // tn),
        in_specs=in_specs,
        out_specs=pl.BlockSpec((tm, tn), lambda i, j: (i, j)),
        compiler_params=_cparams(("arbitrary", "arbitrary"), cfg.vmem_mib),
        name=name,
    )(*args)


def _softplus(x):
    return jnp.maximum(x, 0.0) + jnp.log(1.0 + jnp.exp(-jnp.abs(x)))


def _in_bg_kernel(h_ref, w_ref, alog_ref, dtb_ref, o_ref, *, nh, cs):
    acc = _dot(h_ref[...], w_ref[...])
    tm = acc.shape[0]
    lane = lax.broadcasted_iota(jnp.int32, (1, LANES), 1)
    is_beta = lane < nh
    is_g = jnp.logical_and(lane >= nh, lane < 2 * nh)
    beta = _sigmoid(acc)
    g = -jnp.exp(alog_ref[...]) * _softplus(acc + dtb_ref[...])
    g = jnp.where(is_g, g, 0.0)
    row = lax.broadcasted_iota(jnp.int32, (cs, cs), 0)
    col = lax.broadcasted_iota(jnp.int32, (cs, cs), 1)
    tri = (row >= col).astype(F32)
    for c in range(tm // cs):
        sl = slice(c * cs, (c + 1) * cs)
        gc = jnp.dot(tri, g[sl], preferred_element_type=F32, precision=lax.Precision.HIGHEST)
        o_ref[sl, :] = jnp.where(is_beta, beta[sl], gc)


def _in_bg(h, w_ba, alog_row, dtb_row, cfg):
    n, d = h.shape
    tm = cfg.tm
    return pl.pallas_call(
        functools.partial(_in_bg_kernel, nh=cfg.dn_heads, cs=cfg.delta_cs),
        out_shape=jax.ShapeDtypeStruct((n, LANES), F32),
        grid=(n // tm,),
        in_specs=[pl.BlockSpec((tm, d), lambda i: (i, 0)),
                  pl.BlockSpec((d, LANES), lambda i: (0, 0)),
                  pl.BlockSpec((1, LANES), lambda i: (0, 0)),
                  pl.BlockSpec((1, LANES), lambda i: (0, 0))],
        out_specs=pl.BlockSpec((tm, LANES), lambda i: (i, 0)),
        compiler_params=_cparams(("arbitrary",), cfg.vmem_mib),
        name="in_bg",
    )(h, w_ba, alog_row, dtb_row)


def _in_a_kernel(h_ref, w_ref, cw_ref, o_ref, buf, carry, *, tpb, n_q_tiles, n_qk_tiles, kc, qscale):
    i = pl.program_id(0)
    j = pl.program_id(1)
    tm, tn = o_ref.shape
    acc = _dot(h_ref[...], w_ref[...])

    @pl.when(i % tpb == 0)
    def _():
        carry[j] = jnp.zeros((SUBLANES, tn), F32)

    buf[0:SUBLANES, :] = carry[j]
    buf[SUBLANES:SUBLANES + tm, :] = acc
    carry[j] = buf[tm:tm + SUBLANES, :]
    y = cw_ref[0:1, :] * buf[pl.ds(SUBLANES - (kc - 1), tm), :]
    for k in range(1, kc):
        y = y + cw_ref[k:k + 1, :] * buf[pl.ds(SUBLANES - (kc - 1) + k, tm), :]
    y = _silu(y)
    for hh in range(tn // HEAD_DIM):
        yh = y[:, hh * HEAD_DIM:(hh + 1) * HEAD_DIM]
        r = lax.rsqrt(jnp.sum(yh * yh, axis=-1, keepdims=True) + NORM_EPS)
        mult = jnp.where(j < n_q_tiles, r * qscale, jnp.where(j < n_qk_tiles, r, 1.0))
        o_ref[:, hh * HEAD_DIM:(hh + 1) * HEAD_DIM] = (yh * mult).astype(o_ref.dtype)


def _in_a(h, w, conv_w, seq, cfg):
    n, d = h.shape
    cols = w.shape[1]
    width = cols // 3
    tm, tn = cfg.tm, cfg.tn
    nj = cols // tn
    return pl.pallas_call(
        functools.partial(_in_a_kernel, tpb=seq // tm, n_q_tiles=width // tn, n_qk_tiles=2 * width // tn,
                          kc=cfg.dn_conv, qscale=HEAD_DIM ** -0.5),
        out_shape=jax.ShapeDtypeStruct((n, cols), BF16),
        grid=(n // tm, nj),
        in_specs=[pl.BlockSpec((tm, d), lambda i, j: (i, 0)),
                  pl.BlockSpec((d, tn), lambda i, j: (0, j)),
                  pl.BlockSpec((cfg.dn_conv, tn), lambda i, j: (0, j))],
        out_specs=pl.BlockSpec((tm, tn), lambda i, j: (i, j)),
        scratch_shapes=[pltpu.VMEM((tm + SUBLANES, tn), F32),
                        pltpu.VMEM((nj, SUBLANES, tn), F32)],
        compiler_params=_cparams(("arbitrary", "arbitrary"), cfg.vmem_mib),
        name="in_a",
    )(h, w, conv_w)


def _rope_kernel(pos_ref, freq_ref, sgn_ref, cos_ref, sin_ref):
    ang = pos_ref[...].astype(F32) * freq_ref[...]
    cos_ref[...] = jnp.cos(ang)
    sin_ref[...] = jnp.sin(ang) * sgn_ref[...]


def _rope_tables(pos_col, cfg):
    n = pos_col.shape[0]
    half = ROPE_DIM // 2
    inv_freq = ROPE_THETA ** (-(np.arange(half, dtype=np.float32) * 2.0 / ROPE_DIM))
    freq = np.zeros((1, LANES), np.float32)
    freq[0, :half] = inv_freq
    freq[0, half:ROPE_DIM] = inv_freq
    sgn = np.zeros((1, LANES), np.float32)
    sgn[0, :half] = -1.0
    sgn[0, half:ROPE_DIM] = 1.0
    tm = cfg.tm
    return pl.pallas_call(
        _rope_kernel,
        out_shape=(jax.ShapeDtypeStruct((n, LANES), F32), jax.ShapeDtypeStruct((n, LANES), F32)),
        grid=(n // tm,),
        in_specs=[pl.BlockSpec((tm, 1), lambda i: (i, 0)),
                  pl.BlockSpec((1, LANES), lambda i: (0, 0)),
                  pl.BlockSpec((1, LANES), lambda i: (0, 0))],
        out_specs=(pl.BlockSpec((tm, LANES), lambda i: (i, 0)), pl.BlockSpec((tm, LANES), lambda i: (i, 0))),
        compiler_params=_cparams(("arbitrary",), cfg.vmem_mib),
        name="rope_tables",
    )(pos_col, jnp.asarray(freq), jnp.asarray(sgn))


def _in_b_kernel(h_ref, w_ref, cos_ref, sin_ref, qg_ref, kg_ref, o_ref, ybuf, *, dil, nheads):
    j = pl.program_id(1)
    tm = h_ref.shape[0]
    acc = _dot(h_ref[...], w_ref[...])

    @pl.when(j == 2)
    def _():
        for hh in range(nheads):
            ybuf[hh] = acc[:, hh * HEAD_DIM:(hh + 1) * HEAD_DIM]

    @pl.when(j < 2)
    def _():
        gain = jnp.where(j == 0, qg_ref[...], kg_ref[...])
        cosf = cos_ref[...]
        sinf = sin_ref[...]
        lane = lax.broadcasted_iota(jnp.int32, (1, LANES), 1)
        half = ROPE_DIM // 2
        for hh in range(nheads):
            xh = acc[:, hh * HEAD_DIM:(hh + 1) * HEAD_DIM]
            ms = jnp.mean(xh * xh, axis=-1, keepdims=True)
            xn = xh * lax.rsqrt(ms + NORM_EPS) * gain
            rot = jnp.where(lane < half, pltpu.roll(xn, LANES - half, 1), pltpu.roll(xn, half, 1))
            ybuf[hh] = xn * cosf + rot * sinf

    n = tm // dil
    for hh in range(nheads):
        if dil == 1:
            o_ref[hh, 0] = ybuf[hh].astype(o_ref.dtype)
        else:
            def body(r, c, hh=hh):
                o_ref[hh, r] = ybuf[hh, pl.ds(r, n, stride=dil), :].astype(o_ref.dtype)
                return c
            lax.fori_loop(0, dil, body, 0)


def _in_b(h, w, cosf, sinf, qg, kg, batch, seq, dil, cfg):
    n, d = h.shape
    nheads = cfg.sw_heads
    tn = nheads * HEAD_DIM
    tm = cfg.tm
    tpb = seq // tm
    return pl.pallas_call(
        functools.partial(_in_b_kernel, dil=dil, nheads=nheads),
        out_shape=jax.ShapeDtypeStruct((batch, 3, nheads, dil, seq // dil, HEAD_DIM), BF16),
        grid=(n // tm, 3),
        in_specs=[pl.BlockSpec((tm, d), lambda i, j: (i, 0)),
                  pl.BlockSpec((d, tn), lambda i, j: (0, j)),
                  pl.BlockSpec((tm, LANES), lambda i, j: (i, 0)),
                  pl.BlockSpec((tm, LANES), lambda i, j: (i, 0)),
                  pl.BlockSpec((1, LANES), lambda i, j: (0, 0)),
                  pl.BlockSpec((1, LANES), lambda i, j: (0, 0))],
        out_specs=pl.BlockSpec((None, None, nheads, dil, tm // dil, HEAD_DIM),
                               lambda i, j: (i // tpb, j, 0, 0, i % tpb, 0)),
        scratch_shapes=[pltpu.VMEM((nheads, tm, HEAD_DIM), F32)],
        compiler_params=_cparams(("arbitrary", "arbitrary"), cfg.vmem_mib),
        name=f"in_b_d{dil}",
    )(h, w, cosf, sinf, qg, kg)


def _unit_lower_inverse(mats, cs):
    row = lax.broadcasted_iota(jnp.int32, (cs, cs), 0)
    col = lax.broadcasted_iota(jnp.int32, (cs, cs), 1)
    eye = (row == col).astype(F32)
    base = SUBLANES
    diag = row // base == col // base
    ad = [jnp.where(diag, a, 0.0).astype(BF16) for a in mats]
    a2 = [_dot(d, d) for d in ad]
    a2h = [m.astype(BF16) for m in a2]
    a4 = [_dot(m, m) for m in a2h]
    p = [_dot((eye - d).astype(BF16), (eye + m).astype(BF16)) for d, m in zip(ad, a2)]
    x = [_dot(pp.astype(BF16), (eye + m).astype(BF16)) for pp, m in zip(p, a4)]
    s = base
    while s < cs:
        off = jnp.logical_and(row // (2 * s) == col // (2 * s),
                              jnp.logical_and((row // s) % 2 == 1, (col // s) % 2 == 0))
        xh = [m.astype(BF16) for m in x]
        t = [_dot(m, jnp.where(off, a, 0.0).astype(BF16)) for m, a in zip(xh, mats)]
        x = [m - _dot(tt.astype(BF16), mh) for m, tt, mh in zip(x, t, xh)]
        s *= 2
    return x


def _delta_kernel(q_ref, k_ref, v_ref, bg_ref, z_ref, gain_ref, o_ref,
                  s_ref, bgt_s, u_s, w_s, qd_s, kd_s, aqk_s, egl_s, *, hb, nh, cs, cu):
    hg = pl.program_id(1)
    t = pl.program_id(2)
    tt = q_ref.shape[0]
    nchunks = tt // cs

    @pl.when(t == 0)
    def _():
        s_ref[...] = jnp.zeros(s_ref.shape, F32)

    row = lax.broadcasted_iota(jnp.int32, (cs, cs), 0)
    col = lax.broadcasted_iota(jnp.int32, (cs, cs), 1)
    tril = row >= col
    strict = row > col
    lane = lax.broadcasted_iota(jnp.int32, (1, LANES), 1)

    def prep(ci, carry):
        items = [(cc, hh) for cc in range(cu) for hh in range(hb)]
        r0s = [pl.multiple_of((ci * cu + cc) * cs, cs) for cc in range(cu)]
        for cc in range(cu):
            bgt_s[cc] = bg_ref[pl.ds(r0s[cc], cs), :].T
        kk, qk, gam, rhs_u, rhs_w = [], [], [], [], []
        for cc, hh in items:
            r0 = r0s[cc]
            head = hg * hb + hh
            cols = slice(hh * HEAD_DIM, (hh + 1) * HEAD_DIM)
            bg = bg_ref[pl.ds(r0, cs), :]
            q16 = q_ref[pl.ds(r0, cs), cols]
            k16 = k_ref[pl.ds(r0, cs), cols]
            q = q16.astype(F32)
            k = k16.astype(F32)
            v = v_ref[pl.ds(r0, cs), cols].astype(F32)
            beta = jnp.sum(jnp.where(lane == head, bg, 0.0), axis=-1, keepdims=True)
            gc = jnp.sum(jnp.where(lane == nh + head, bg, 0.0), axis=-1, keepdims=True)
            gl = gc[cs - 1:cs, :]
            gcr = bgt_s[cc, pl.ds(nh + head, 1), :]
            eg = jnp.exp(gc)
            kb = k * beta
            gam.append(jnp.where(tril, jnp.exp(gc - gcr), 0.0))
            kk.append(_dot_nt(kb.astype(BF16), k16))
            qk.append(_dot_nt(q16, k16))
            rhs_u.append((v * beta).astype(BF16))
            rhs_w.append((kb * eg).astype(BF16))
            qd_s[hh, pl.ds(r0, cs), :] = (q * eg).astype(BF16)
            kd_s[hh, pl.ds(r0, cs), :] = (k * jnp.exp(gl - gc)).astype(BF16)
            egl_s[hh, ci * cu + cc] = jnp.broadcast_to(jnp.exp(gl), (1, LANES))
        a = [jnp.where(strict, m * g, 0.0) for m, g in zip(kk, gam)]
        for (cc, hh), m, g in zip(items, qk, gam):
            aqk_s[hh, pl.ds(r0s[cc], cs), :] = (m * g).astype(BF16)
        x16 = [m.astype(BF16) for m in _unit_lower_inverse(a, cs)]
        for (cc, hh), m, ru, rw in zip(items, x16, rhs_u, rhs_w):
            u_s[hh, pl.ds(r0s[cc], cs), :] = _dot(m, ru)
            w_s[hh, pl.ds(r0s[cc], cs), :] = _dot(m, rw).astype(BF16)
        return carry

    lax.fori_loop(0, nchunks // cu, prep, 0)

    gain = gain_ref[...]

    def scan(c, carry):
        r0 = pl.multiple_of(c * cs, cs)
        rows = pl.ds(r0, cs)
        heads = range(hb)
        state = [s_ref[hh] for hh in heads]
        s16 = [m.astype(BF16) for m in state]
        ws = [_dot(w_s[hh, rows, :], s16[hh]) for hh in heads]
        qs = [_dot(qd_s[hh, rows, :], s16[hh]) for hh in heads]
        vn16 = [(u_s[hh, rows, :] - ws[hh]).astype(BF16) for hh in heads]
        av = [_dot(aqk_s[hh, rows, :], vn16[hh]) for hh in heads]
        kv = [_dot_tn(kd_s[hh, rows, :], vn16[hh]) for hh in heads]
        for hh in heads:
            cols = slice(hh * HEAD_DIM, (hh + 1) * HEAD_DIM)
            s_ref[hh] = state[hh] * egl_s[hh, c] + kv[hh]
            o = qs[hh] + av[hh]
            ms = jnp.mean(o * o, axis=-1, keepdims=True)
            on = o * lax.rsqrt(ms + NORM_EPS) * gain * z_ref[rows, cols].astype(F32)
            o_ref[rows, cols] = on.astype(o_ref.dtype)
        return carry

    lax.fori_loop(0, nchunks, scan, 0)


def _delta(qkv, bg, z, gain_row, batch, seq, cfg):
    n, cols3 = qkv.shape
    width = cols3 // 3
    hb, tt, cs, cu = cfg.delta_hb, cfg.delta_tt, cfg.delta_cs, cfg.delta_cu
    bw = hb * HEAD_DIM
    ngroups = width // bw
    tpb = seq // tt
    nchunks = tt // cs
    blk = lambda off: pl.BlockSpec((tt, bw), lambda b, g, t: (b * tpb + t, off + g))
    return pl.pallas_call(
        functools.partial(_delta_kernel, hb=hb, nh=cfg.dn_heads, cs=cs, cu=cu),
        out_shape=jax.ShapeDtypeStruct((n, width), BF16),
        grid=(batch, ngroups, tpb),
        in_specs=[blk(0), blk(ngroups), blk(2 * ngroups),
                  pl.BlockSpec((tt, LANES), lambda b, g, t: (b * tpb + t, 0)),
                  blk(0),
                  pl.BlockSpec((1, LANES), lambda b, g, t: (0, 0))],
        out_specs=blk(0),
        scratch_shapes=[pltpu.VMEM((hb, HEAD_DIM, HEAD_DIM), F32),
                        pltpu.VMEM((cu, LANES, cs), F32),
                        pltpu.VMEM((hb, tt, HEAD_DIM), F32),
                        pltpu.VMEM((hb, tt, HEAD_DIM), BF16),
                        pltpu.VMEM((hb, tt, HEAD_DIM), BF16),
                        pltpu.VMEM((hb, tt, HEAD_DIM), BF16),
                        pltpu.VMEM((hb, tt, cs), BF16),
                        pltpu.VMEM((hb, nchunks, 1, LANES), F32)],
        compiler_params=_cparams(("arbitrary", "arbitrary", "arbitrary"), cfg.vmem_mib),
        name="delta_rule",
    )(qkv, qkv, qkv, bg, z, gain_row)


ATT_BLK = 128


def _attn_kernel(*refs, dils, seq, gb):
    ng = len(dils)
    qkv_refs = refs[:3 * ng]
    o_ref = refs[3 * ng]
    nat_o, nat_l = refs[3 * ng + 1:]
    blk = ATT_BLK
    scale = HEAD_DIM ** -0.5
    row = lax.broadcasted_iota(jnp.int32, (blk, blk), 0)
    col = lax.broadcasted_iota(jnp.int32, (blk, blk), 1)
    mask_cur = col <= row
    mask_prev = col >= row
    neg = -jnp.inf

    for g, dil in enumerate(dils):
        q_ref, k_ref, v_ref = qkv_refs[3 * g:3 * g + 3]
        nb = seq // dil // blk

        def body(it, carry, g=g, dil=dil, nb=nb, q_ref=q_ref, k_ref=k_ref, v_ref=v_ref):
            items = range(gb)
            idx = [it * gb + gi for gi in items]
            rs = [i // nb for i in idx]
            bs = [i % nb for i in idx]
            c0 = [pl.multiple_of(b * blk, blk) for b in bs]
            p0 = [pl.multiple_of(jnp.maximum(b - 1, 0) * blk, blk) for b in bs]
            q = [q_ref[rs[i], pl.ds(c0[i], blk), :] for i in items]
            sc = [_dot_nt(q[i], k_ref[rs[i], pl.ds(c0[i], blk), :]) * scale for i in items]
            sp = [_dot_nt(q[i], k_ref[rs[i], pl.ds(p0[i], blk), :]) * scale for i in items]
            sc = [jnp.where(mask_cur, m, neg) for m in sc]
            sp = [jnp.where(jnp.logical_and(mask_prev, bs[i] > 0), sp[i], neg) for i in items]
            mx = [jnp.maximum(jnp.max(sc[i], axis=-1, keepdims=True), jnp.max(sp[i], axis=-1, keepdims=True))
                  for i in items]
            pc = [jnp.exp(sc[i] - mx[i]) for i in items]
            pp = [jnp.exp(sp[i] - mx[i]) for i in items]
            den = [jnp.sum(pc[i], axis=-1, keepdims=True) + jnp.sum(pp[i], axis=-1, keepdims=True) for i in items]
            inv = [1.0 / d for d in den]
            o = [_dot((pc[i] * inv[i]).astype(BF16), v_ref[rs[i], pl.ds(c0[i], blk), :])
                 + _dot((pp[i] * inv[i]).astype(BF16), v_ref[rs[i], pl.ds(p0[i], blk), :]) for i in items]
            for i in items:
                lse = mx[i] + jnp.log(den[i])
                start = rs[i] + dil * c0[i]
                if dil == 1:
                    rows = pl.ds(pl.multiple_of(start, blk), blk)
                else:
                    rows = pl.ds(start, blk, stride=dil)
                nat_o[g, rows, :] = o[i]
                nat_l[g, rows, :] = jnp.broadcast_to(lse, (blk, LANES))
            return carry

        lax.fori_loop(0, dil * nb // gb, body, 0)

    mrows = 256

    def merge(i, carry):
        rows = pl.ds(pl.multiple_of(i * mrows, mrows), mrows)
        ls = [nat_l[g, rows, :] for g in range(ng)]
        m = functools.reduce(jnp.maximum, ls)
        es = [jnp.exp(l - m) for l in ls]
        inv = 1.0 / functools.reduce(lambda a, b: a + b, es)
        o = functools.reduce(lambda a, b: a + b, [(es[g] * inv) * nat_o[g, rows, :] for g in range(ng)])
        o_ref[rows, :] = o.astype(o_ref.dtype)
        return carry

    lax.fori_loop(0, seq // mrows, merge, 0)


def _attn(qkvs, dils, batch, seq, cfg):
    nheads = cfg.sw_heads
    ng = len(dils)
    in_specs, args = [], []
    for arr, dil in zip(qkvs, dils):
        for s in range(3):
            in_specs.append(pl.BlockSpec((None, None, None, dil, seq // dil, HEAD_DIM),
                                         lambda b, h, s=s: (b, s, h, 0, 0, 0)))
            args.append(arr)
    return pl.pallas_call(
        functools.partial(_attn_kernel, dils=tuple(dils), seq=seq, gb=cfg.attn_gb),
        out_shape=jax.ShapeDtypeStruct((batch * seq, nheads * HEAD_DIM), BF16),
        grid=(batch, nheads),
        in_specs=in_specs,
        out_specs=pl.BlockSpec((seq, HEAD_DIM), lambda b, h: (b, h)),
        scratch_shapes=[pltpu.VMEM((ng, seq, HEAD_DIM), F32),
                        pltpu.VMEM((ng, seq, LANES), F32)],
        compiler_params=_cparams(("arbitrary", "arbitrary"), cfg.vmem_mib),
        name="swa",
    )(*args)


def _merge_kernel(oa_ref, ob_ref, wa_ref, wb_ref, ga_ref, gb_ref, o_ref):
    ya = _dot(oa_ref[...], wa_ref[...])
    yb = _dot(ob_ref[...], wb_ref[...])
    o_ref[...] = (ga_ref[...].astype(F32) * ya + gb_ref[...].astype(F32) * yb).astype(o_ref.dtype)


def _merge(oa, ob, wa, wb, gate, cfg):
    n, ka = oa.shape
    kb = ob.shape[1]
    d = wa.shape[1]
    tm, tn = cfg.tm, cfg.tn
    nj = d // tn
    return pl.pallas_call(
        _merge_kernel,
        out_shape=jax.ShapeDtypeStruct((n, d), BF16),
        grid=(n // tm, nj),
        in_specs=[pl.BlockSpec((tm, ka), lambda i, j: (i, 0)),
                  pl.BlockSpec((tm, kb), lambda i, j: (i, 0)),
                  pl.BlockSpec((ka, tn), lambda i, j: (0, j)),
                  pl.BlockSpec((kb, tn), lambda i, j: (0, j)),
                  pl.BlockSpec((tm, tn), lambda i, j: (i, j)),
                  pl.BlockSpec((tm, tn), lambda i, j: (i, nj + j))],
        out_specs=pl.BlockSpec((tm, tn), lambda i, j: (i, j)),
        compiler_params=_cparams(("arbitrary", "arbitrary"), cfg.vmem_mib),
        name="merge",
    )(oa, ob, wa, wb, gate, gate)


def _oproj_kernel(m_ref, w_ref, x_ref, mod_ref, o_ref, *, gate_idx):
    y = _dot(m_ref[...], w_ref[...])
    o_ref[...] = x_ref[...] + mod_ref[gate_idx:gate_idx + 1, :] * y


def _oproj(m, w_o, x2, mod3, seq, gate_idx, cfg):
    n, d = m.shape
    tm, tn = cfg.tm, cfg.tn
    tpb = seq // tm
    return pl.pallas_call(
        functools.partial(_oproj_kernel, gate_idx=gate_idx),
        out_shape=jax.ShapeDtypeStruct((n, d), F32),
        grid=(n // tm, d // tn),
        in_specs=[pl.BlockSpec((tm, d), lambda i, j: (i, 0)),
                  pl.BlockSpec((d, tn), lambda i, j: (0, j)),
                  pl.BlockSpec((tm, tn), lambda i, j: (i, j)),
                  pl.BlockSpec((None, N_MOD, tn), lambda i, j: (i // tpb, 0, j))],
        out_specs=pl.BlockSpec((tm, tn), lambda i, j: (i, j)),
        compiler_params=_cparams(("arbitrary", "arbitrary"), cfg.vmem_mib),
        name="oproj",
    )(m, w_o, x2, mod3)


def _ffn_kernel(h_ref, wg_ref, wv_ref, cg_ref, cv_ref, wd_ref, x_ref, mod_ref, o_ref,
                bufg, bufv, carry, act_a, act_b, *, tpb, kc, gate_idx, nj):
    i = pl.program_id(0)
    j = pl.program_id(1)
    tm = h_ref.shape[0]

    @pl.when(j == 0)
    def _():
        o_ref[...] = jnp.zeros(o_ref.shape, F32)
        act_b[...] = jnp.zeros(act_b.shape, BF16)

    @pl.when(jnp.logical_and(j == 0, i % tpb == 0))
    def _():
        carry[...] = jnp.zeros(carry.shape, F32)

    def conv(buf, which, acc, c_ref):
        buf[0:SUBLANES, :] = carry[j, which]
        buf[SUBLANES:SUBLANES + tm, :] = acc
        carry[j, which] = buf[tm:tm + SUBLANES, :]
        y = c_ref[0:1, :] * buf[pl.ds(SUBLANES - (kc - 1), tm), :]
        for k in range(1, kc):
            y = y + c_ref[k:k + 1, :] * buf[pl.ds(SUBLANES - (kc - 1) + k, tm), :]
        return y

    def step(act_rd, act_wr):
        h = h_ref[...]
        g = _dot(h, wg_ref[...])
        v = _dot(h, wv_ref[...])
        o_ref[...] += _dot(act_rd[...], wd_ref[...])
        act_wr[...] = (_silu(conv(bufg, 0, g, cg_ref)) * conv(bufv, 1, v, cv_ref)).astype(BF16)

    @pl.when(jnp.logical_and(j < nj, j % 2 == 0))
    def _():
        step(act_b, act_a)

    @pl.when(jnp.logical_and(j < nj, j % 2 == 1))
    def _():
        step(act_a, act_b)

    @pl.when(j == nj)
    def _():
        last = act_a if nj % 2 == 1 else act_b
        y = o_ref[...] + _dot(last[...], wd_ref[...])
        o_ref[...] = x_ref[...] + mod_ref[gate_idx:gate_idx + 1, :] * y


def _ffn(h, w_up, conv_w, w_down, x1, mod3, seq, gate_idx, cfg):
    n, d = h.shape
    f = w_down.shape[0]
    tm, tf = cfg.tm_ffn, cfg.tf_ffn
    nj = f // tf
    tpb = seq // tm
    kc = cfg.ffn_conv
    up = lambda j: jnp.minimum(j, nj - 1)
    down = lambda j: jnp.maximum(j - 1, 0)
    return pl.pallas_call(
        functools.partial(_ffn_kernel, tpb=tpb, kc=kc, gate_idx=gate_idx, nj=nj),
        out_shape=jax.ShapeDtypeStruct((n, d), F32),
        grid=(n // tm, nj + 1),
        in_specs=[pl.BlockSpec((tm, d), lambda i, j: (i, 0)),
                  pl.BlockSpec((d, tf), lambda i, j: (0, up(j))),
                  pl.BlockSpec((d, tf), lambda i, j: (0, nj + up(j))),
                  pl.BlockSpec((kc, tf), lambda i, j: (0, up(j))),
                  pl.BlockSpec((kc, tf), lambda i, j: (0, nj + up(j))),
                  pl.BlockSpec((tf, d), lambda i, j: (down(j), 0)),
                  pl.BlockSpec((tm, d), lambda i, j: (i, 0), pipeline_mode=pl.Buffered(1)),
                  pl.BlockSpec((None, N_MOD, d), lambda i, j: (i // tpb, 0, 0))],
        out_specs=pl.BlockSpec((tm, d), lambda i, j: (i, 0)),
        scratch_shapes=[pltpu.VMEM((tm + SUBLANES, tf), F32),
                        pltpu.VMEM((tm + SUBLANES, tf), F32),
                        pltpu.VMEM((nj, 2, SUBLANES, tf), F32),
                        pltpu.VMEM((tm, tf), BF16),
                        pltpu.VMEM((tm, tf), BF16)],
        compiler_params=_cparams(("arbitrary", "arbitrary"), cfg.vmem_mib),
        name="ffn",
    )(h, w_up, w_up, conv_w, conv_w, w_down, x1, mod3)


def _row(v, offset=0):
    return jnp.zeros((1, LANES), F32).at[0, offset:offset + v.shape[0]].set(v.astype(F32))


def _forward(cfg, x, c, positions, w_ada, b_ada, w_in, b_gate, conv_qkv, a_log, dt_bias, o_norm_gain,
             w_a_proj, q_norm_gain, k_norm_gain, w_b_proj, w_o, w_up, conv_ffn, w_down):
    batch, seq, d = x.shape
    n = batch * seq
    depth = w_ada.shape[0]
    nh = cfg.dn_heads
    dn_width = nh * HEAD_DIM
    sw_width = cfg.sw_heads * HEAD_DIM
    ng = len(cfg.sw_groups)
    dils = [dl for _, dl in cfg.sw_groups]
    assert all(win // dl == ATT_BLK for win, dl in cfg.sw_groups)
    assert 2 * nh <= LANES
    splits = np.cumsum([3 * dn_width, dn_width, nh, nh, 3 * ng * sw_width, 2 * d])
    assert w_in.shape[2] == splits[-1]

    x2 = x.reshape(n, d)
    pos_col = positions.reshape(n, 1)
    c_pad = jnp.zeros((2 * SUBLANES, d), F32).at[:batch].set(c)
    cosf, sinf = _rope_tables(pos_col, cfg)

    for layer in range(depth):
        wl = w_in[layer]
        w_qkv_a = wl[:, :splits[0]].astype(BF16)
        w_z = wl[:, splits[0]:splits[1]].astype(BF16)
        w_ba = jnp.zeros((d, LANES), BF16).at[:, :2 * nh].set(wl[:, splits[1]:splits[3]].astype(BF16))
        w_qkv_b = [wl[:, splits[3] + g * 3 * sw_width:splits[3] + (g + 1) * 3 * sw_width].astype(BF16)
                   for g in range(ng)]
        w_gate = wl[:, splits[4]:].astype(BF16)
        w_a16 = w_a_proj[layer].astype(BF16)
        w_b16 = w_b_proj[layer].astype(BF16)
        w_o16 = w_o[layer].astype(BF16)
        w_up16 = w_up[layer].astype(BF16)
        w_down16 = w_down[layer].astype(BF16)

        mod = _mod(c_pad, w_ada[layer], b_ada[layer][None, :], cfg)[:batch]
        mod3 = mod.reshape(batch, N_MOD, d)

        h = _norm_mod(x2, mod3, seq, 0, 1, cfg)
        qkv_a = _in_a(h, w_qkv_a, conv_qkv[layer], seq, cfg)
        bg = _in_bg(h, w_ba, _row(a_log[layer], nh), _row(dt_bias[layer], nh), cfg)
        z = _mm_act(h, w_z, None, "silu", cfg, "in_z")
        gate = _mm_act(h, w_gate, b_gate[layer][None, :], "sigmoid_bias", cfg, "in_gate")
        qg = _row(q_norm_gain[layer])
        kg = _row(k_norm_gain[layer])
        qkv_b = [_in_b(h, w_qkv_b[g], cosf, sinf, qg, kg, batch, seq, dils[g], cfg) for g in range(ng)]

        o_a = _delta(qkv_a, bg, z, _row(o_norm_gain[layer]), batch, seq, cfg)
        o_b = _attn(qkv_b, dils, batch, seq, cfg)

        m = _merge(o_a, o_b, w_a16, w_b16, gate, cfg)
        x2 = _oproj(m, w_o16, x2, mod3, seq, 2, cfg)
        h2 = _norm_mod(x2, mod3, seq, 3, 4, cfg)
        x2 = _ffn(h2, w_up16, conv_ffn[layer], w_down16, x2, mod3, seq, 5, cfg)
    return x2.reshape(batch, seq, d)


def kernel(x, c, positions, w_ada, b_ada, w_in, b_gate, conv_qkv, a_log, dt_bias, o_norm_gain, w_a_proj,
           q_norm_gain, k_norm_gain, w_b_proj, w_o, w_up, conv_ffn, w_down):
    return _forward(PROD, x, c, positions, w_ada, b_ada, w_in, b_gate, conv_qkv, a_log, dt_bias, o_norm_gain,
                    w_a_proj, q_norm_gain, k_norm_gain, w_b_proj, w_o, w_up, conv_ffn, w_down)
```
